```python
import math
import jax, jax.numpy as jnp
from jax import lax
import numpy as np

D_MODEL = 1024
BATCH = 1
SEQ = 16384
DEPTH = 1
DEC_BATCH = 128
DEC_SEQ = 1
PAST_LEN = 16384
PAGE_SIZE = 128

HEAD_DIM = 64
DIFF_HEADS = 4
DIFF_WIDTH = DIFF_HEADS * 2 * HEAD_DIM
MLA_HEADS = 8
MLA_NOPE = 64
MLA_ROPE = 32
MLA_V = 64
Q_LORA = 256
KV_LORA = 128
MLA_WIDTH = MLA_HEADS * MLA_V
MIX_WIDTH = DIFF_WIDTH + MLA_WIDTH
D_FF = -(-8 * D_MODEL // (3 * 256)) * 256
PLE_DIM = 256
NUM_BUCKETS = 32
MAX_DISTANCE = 128
ROPE_THETA = 10000.0
Q_BLOCK = 128
EPS = 1e-6
NEG = -1e30
DIFF_SCALE = HEAD_DIM ** -0.5
MLA_SCALE = (MLA_NOPE + MLA_ROPE) ** -0.5
IN_SPLITS = (DIFF_WIDTH, 2 * DIFF_WIDTH, 3 * DIFF_WIDTH, 3 * DIFF_WIDTH + Q_LORA,
             3 * DIFF_WIDTH + Q_LORA + KV_LORA)
IN_WIDTH = 3 * DIFF_WIDTH + Q_LORA + KV_LORA + MLA_ROPE

kernel_name = "hymba_diffattn_mla_decoder_step"


def _lambda_init(layer):
    return 0.8 - 0.6 * math.exp(-0.3 * layer)


def _rms_norm(x, g):
    xf = x.astype(jnp.float32)
    y = xf * lax.rsqrt(jnp.mean(xf * xf, axis=-1, keepdims=True) + EPS)
    return (y * g.astype(jnp.float32)).astype(x.dtype)


def _t5_bias(rel, table):
    n = jnp.maximum(rel, 0)
    max_exact = NUM_BUCKETS // 2
    nf = jnp.maximum(n, 1).astype(jnp.float32)
    large = max_exact + (jnp.log(nf / max_exact) / math.log(MAX_DISTANCE / max_exact)
                         * (NUM_BUCKETS - max_exact)).astype(jnp.int32)
    large = jnp.minimum(large, NUM_BUCKETS - 1)
    bucket = jnp.where(n < max_exact, n, large)
    return jnp.transpose(table.astype(jnp.float32)[bucket], (2, 0, 1))


def _rope(x, pos):
    d = x.shape[-1]
    half = d // 2
    inv = ROPE_THETA ** (-jnp.arange(half, dtype=jnp.float32) * 2.0 / d)
    ang = pos.astype(jnp.float32)[:, None] * inv[None, :]
    cos = jnp.cos(ang)[None, :, None, :]
    sin = jnp.sin(ang)[None, :, None, :]
    xf = x.astype(jnp.float32)
    x1, x2 = xf[..., :half], xf[..., half:]
    return jnp.concatenate([x1 * cos - x2 * sin, x1 * sin + x2 * cos], axis=-1).astype(x.dtype)


def _project(hn, pos, w_in, q_norm, w_q_up, kv_norm, w_uk):
    B, S, _ = hn.shape
    z = hn @ w_in
    qd, kd, vd, cq, ckv, kr = jnp.split(z, IN_SPLITS, axis=-1)
    q_diff = qd.reshape(B, S, DIFF_HEADS, 2 * HEAD_DIM)
    k_diff = kd.reshape(B, S, DIFF_HEADS, 2 * HEAD_DIM)
    v_diff = vd.reshape(B, S, DIFF_HEADS, 2 * HEAD_DIM)
    q = (_rms_norm(cq, q_norm) @ w_q_up).reshape(B, S, MLA_HEADS, MLA_NOPE + MLA_ROPE)
    q_lat = jnp.einsum('bshn,chn->bshc', q[..., :MLA_NOPE], w_uk)
    q_rope = _rope(q[..., MLA_NOPE:], pos)
    c_kv = _rms_norm(ckv, kv_norm)
    k_rope = _rope(kr[:, :, None, :], pos)[:, :, 0, :]
    return q_diff, k_diff, v_diff, q_lat, q_rope, c_kv, k_rope


def _diff_scores(q, k, bias):
    s1 = jnp.einsum('bqhd,bkhd->bhqk', q[..., :HEAD_DIM], k[..., :HEAD_DIM],
                    preferred_element_type=jnp.float32)
    s2 = jnp.einsum('bqhd,bkhd->bhqk', q[..., HEAD_DIM:], k[..., HEAD_DIM:],
                    preferred_element_type=jnp.float32)
    return s1 * DIFF_SCALE + bias, s2 * DIFF_SCALE + bias


def _mla_scores(q_lat, q_rope, c, r):
    s = jnp.einsum('bqhc,bkc->bhqk', q_lat, c, preferred_element_type=jnp.float32)
    s = s + jnp.einsum('bqhr,bkr->bhqk', q_rope, r, preferred_element_type=jnp.float32)
    return s * MLA_SCALE


def _diff_lambda(lam_params, layer):
    lp = lam_params.astype(jnp.float32)
    return jnp.exp(jnp.sum(lp[0] * lp[1])) - jnp.exp(jnp.sum(lp[2] * lp[3])) + _lambda_init(layer)


def _attend_prompt(q_diff, k_diff, v_diff, q_lat, q_rope, c_kv, k_rope, lam, rel_bias):
    B, S = q_diff.shape[:2]
    kpos = jnp.arange(S)
    vf = v_diff.astype(jnp.float32)
    cf = c_kv.astype(jnp.float32)

    def block(i):
        start = i * Q_BLOCK
        sl = lambda a: lax.dynamic_slice_in_dim(a, start, Q_BLOCK, axis=1)
        rel = (start + jnp.arange(Q_BLOCK))[:, None] - kpos[None, :]
        mask = rel >= 0
        s1, s2 = _diff_scores(sl(q_diff), k_diff, _t5_bias(rel, rel_bias))
        a = (jax.nn.softmax(jnp.where(mask, s1, NEG), axis=-1)
             - lam * jax.nn.softmax(jnp.where(mask, s2, NEG), axis=-1))
        o_diff = jnp.einsum('bhqk,bkhd->bqhd', a, vf)
        s_m = _mla_scores(sl(q_lat), sl(q_rope), c_kv, k_rope)
        o_lat = jnp.einsum('bhqk,bkc->bqhc', jax.nn.softmax(jnp.where(mask, s_m, NEG), axis=-1), cf)
        return o_diff, o_lat

    o_diff, o_lat = lax.map(block, jnp.arange(S // Q_BLOCK))
    o_diff = jnp.moveaxis(o_diff, 0, 1).reshape(B, S, DIFF_HEADS, 2 * HEAD_DIM)
    o_lat = jnp.moveaxis(o_lat, 0, 1).reshape(B, S, MLA_HEADS, KV_LORA)
    return o_diff, o_lat


def _online(m, l, acc, s, pv):
    m_new = jnp.maximum(m, jnp.max(s, axis=-1))
    alpha = jnp.exp(m - m_new)
    p = jnp.exp(s - m_new[..., None])
    l_new = l * alpha + jnp.sum(p, axis=-1)
    acc_new = acc * jnp.swapaxes(alpha, 1, 2)[..., None] + pv(p)
    return m_new, l_new, acc_new


def _attend_sample(q_diff, k_diff, v_diff, q_lat, q_rope, c_kv, k_rope, lam, rel_bias,
                   cache_k, cache_v, cache_c, cache_r, page_table, layer, past_len):
    Bd, Sd = q_diff.shape[:2]
    P = cache_k.shape[2]
    n_pages = page_table.shape[1]
    qpos = past_len + jnp.arange(Sd)

    def init(h, d):
        return (jnp.full((Bd, h, Sd), NEG, jnp.float32), jnp.zeros((Bd, h, Sd), jnp.float32),
                jnp.zeros((Bd, Sd, h, d), jnp.float32))

    carry0 = (init(DIFF_HEADS, 2 * HEAD_DIM), init(DIFF_HEADS, 2 * HEAD_DIM), init(MLA_HEADS, KV_LORA))

    def absorb(carry, k, v, c, r, rel):
        st1, st2, stm = carry
        mask = rel >= 0
        s1, s2 = _diff_scores(q_diff, k, _t5_bias(rel, rel_bias))
        s_m = _mla_scores(q_lat, q_rope, c, r)
        vf = v.astype(jnp.float32)
        cf = c.astype(jnp.float32)
        pv_d = lambda p: jnp.einsum('bhqk,bkhd->bqhd', p, vf)
        pv_m = lambda p: jnp.einsum('bhqk,bkc->bqhc', p, cf)
        st1 = _online(*st1, jnp.where(mask, s1, NEG), pv_d)
        st2 = _online(*st2, jnp.where(mask, s2, NEG), pv_d)
        stm = _online(*stm, jnp.where(mask, s_m, NEG), pv_m)
        return (st1, st2, stm)

    def step(carry, xs):
        j, phys = xs
        kpos = j * P + jnp.arange(P)
        rel = qpos[:, None] - kpos[None, :]
        carry = absorb(carry, cache_k[layer, phys], cache_v[layer, phys],
                       cache_c[layer, phys], cache_r[layer, phys], rel)
        return carry, None

    carry, _ = lax.scan(step, carry0, (jnp.arange(n_pages), page_table.T))
    carry = absorb(carry, k_diff, v_diff, c_kv, k_rope, qpos[:, None] - qpos[None, :])
    (_, l1, a1), (_, l2, a2), (_, lm, am) = carry
    fin = lambda l, a: a / jnp.swapaxes(l, 1, 2)[..., None]
    o_diff = fin(l1, a1) - lam * fin(l2, a2)
    o_lat = fin(lm, am)
    return o_diff, o_lat


def _trunk_layer(h, p_l, pos, layer, attend, norm_attn, w_in, diff_lambda, diff_subln,
                 mla_q_norm, w_q_up, mla_kv_norm, w_uk, w_uv, w_out, norm_ffn, w_ffn_gate,
                 w_ffn_up, w_ffn_down, norm_ple, w_ple_gate, w_ple_proj):
    B, S, _ = h.shape
    hn = _rms_norm(h, norm_attn)
    q_diff, k_diff, v_diff, q_lat, q_rope, c_kv, k_rope = _project(
        hn, pos, w_in, mla_q_norm, w_q_up, mla_kv_norm, w_uk)
    lam = _diff_lambda(diff_lambda, layer)
    o_diff, o_lat = attend(q_diff, k_diff, v_diff, q_lat, q_rope, c_kv, k_rope, lam)
    o_diff = _rms_norm(o_diff, diff_subln) * (1.0 - _lambda_init(layer))
    o_mla = jnp.einsum('bshc,chv->bshv', o_lat, w_uv.astype(jnp.float32))
    mix = jnp.concatenate([o_diff.reshape(B, S, DIFF_WIDTH), o_mla.reshape(B, S, MLA_WIDTH)],
                          axis=-1).astype(h.dtype)
    h = h + mix @ w_out
    hn = _rms_norm(h, norm_ffn)
    h = h + (jax.nn.silu(hn @ w_ffn_gate) * (hn @ w_ffn_up)) @ w_ffn_down
    gate = jax.nn.sigmoid(_rms_norm(h, norm_ple) @ w_ple_gate)
    h = h + gate * (p_l @ w_ple_proj)
    return h, (k_diff, v_diff, c_kv, k_rope)


def setup_inputs(seed: int = 0) -> dict:
    key = jax.random.key(seed)
    ks = jax.random.split(key, 32)
    f32 = jnp.float32
    nrm = lambda k, shape, scale: jax.random.normal(k, shape, f32) * scale
    gain = lambda k, shape: 1.0 + 0.05 * jax.random.normal(k, shape, f32)
    n_pages = PAST_LEN // PAGE_SIZE
    n_used = DEC_BATCH * n_pages
    n_pool = n_used + max(1, n_used // 4)
    page_table = jax.random.permutation(ks[0], n_pool)[:n_used].reshape(DEC_BATCH, n_pages).astype(jnp.int32)
    return {
        "x_prompt": nrm(ks[1], (BATCH, SEQ, D_MODEL), 1.0),
        "x_sample": nrm(ks[2], (DEC_BATCH, DEC_SEQ, D_MODEL), 1.0),
        "p_prompt": nrm(ks[3], (DEPTH, BATCH, SEQ, PLE_DIM), 1.0),
        "p_sample": nrm(ks[4], (DEPTH, DEC_BATCH, DEC_SEQ, PLE_DIM), 1.0),
        "cache_diff_k": nrm(ks[5], (DEPTH, n_pool, PAGE_SIZE, DIFF_HEADS, 2 * HEAD_DIM), 1.0),
        "cache_diff_v": nrm(ks[6], (DEPTH, n_pool, PAGE_SIZE, DIFF_HEADS, 2 * HEAD_DIM), 1.0),
        "cache_mla_ckv": nrm(ks[7], (DEPTH, n_pool, PAGE_SIZE, KV_LORA), 1.0),
        "cache_mla_krope": nrm(ks[8], (DEPTH, n_pool, PAGE_SIZE, MLA_ROPE), 1.0),
        "page_table": page_table,
        "rel_bias": nrm(ks[9], (NUM_BUCKETS, DIFF_HEADS), 0.5),
        "norm_attn": gain(ks[10], (DEPTH, D_MODEL)),
        "w_in": nrm(ks[11], (DEPTH, D_MODEL, IN_WIDTH), D_MODEL ** -0.5),
        "diff_lambda": nrm(ks[12], (DEPTH, 4, HEAD_DIM), 0.1),
        "diff_subln": gain(ks[13], (DEPTH, 2 * HEAD_DIM)),
        "mla_q_norm": gain(ks[14], (DEPTH, Q_LORA)),
        "w_q_up": nrm(ks[15], (DEPTH, Q_LORA, MLA_HEADS * (MLA_NOPE + MLA_ROPE)), Q_LORA ** -0.5),
        "mla_kv_norm": gain(ks[16], (DEPTH, KV_LORA)),
        "w_uk": nrm(ks[17], (DEPTH, KV_LORA, MLA_HEADS, MLA_NOPE), KV_LORA ** -0.5),
        "w_uv": nrm(ks[18], (DEPTH, KV_LORA, MLA_HEADS, MLA_V), KV_LORA ** -0.5),
        "w_out": nrm(ks[19], (DEPTH, MIX_WIDTH, D_MODEL), MIX_WIDTH ** -0.5),
        "norm_ffn": gain(ks[20], (DEPTH, D_MODEL)),
        "w_ffn_gate": nrm(ks[21], (DEPTH, D_MODEL, D_FF), D_MODEL ** -0.5),
        "w_ffn_up": nrm(ks[22], (DEPTH, D_MODEL, D_FF), D_MODEL ** -0.5),
        "w_ffn_down": nrm(ks[23], (DEPTH, D_FF, D_MODEL), D_FF ** -0.5),
        "norm_ple": gain(ks[24], (DEPTH, D_MODEL)),
        "w_ple_gate": nrm(ks[25], (DEPTH, D_MODEL, D_MODEL), D_MODEL ** -0.5),
        "w_ple_proj": nrm(ks[26], (DEPTH, PLE_DIM, D_MODEL), PLE_DIM ** -0.5),
        "final_norm": gain(ks[27], (D_MODEL,)),
    }


def reference(x_prompt, x_sample, p_prompt, p_sample, cache_diff_k, cache_diff_v, cache_mla_ckv,
              cache_mla_krope, page_table, rel_bias, norm_attn, w_in, diff_lambda, diff_subln,
              mla_q_norm, w_q_up, mla_kv_norm, w_uk, w_uv, w_out, norm_ffn, w_ffn_gate, w_ffn_up,
              w_ffn_down, norm_ple, w_ple_gate, w_ple_proj, final_norm):
    S = x_prompt.shape[1]
    Sd = x_sample.shape[1]
    past_len = page_table.shape[1] * cache_diff_k.shape[2]
    pos_p = jnp.arange(S)
    pos_s = past_len + jnp.arange(Sd)
    hp, hs = x_prompt, x_sample
    new_p, new_s = [], []
    for i in range(DEPTH):
        lw = (norm_attn[i], w_in[i], diff_lambda[i], diff_subln[i], mla_q_norm[i], w_q_up[i],
              mla_kv_norm[i], w_uk[i], w_uv[i], w_out[i], norm_ffn[i], w_ffn_gate[i], w_ffn_up[i],
              w_ffn_down[i], norm_ple[i], w_ple_gate[i], w_ple_proj[i])
        attend_p = lambda *a: _attend_prompt(*a, rel_bias)
        attend_s = lambda *a, i=i: _attend_sample(*a, rel_bias, cache_diff_k, cache_diff_v, cache_mla_ckv,
                                                  cache_mla_krope, page_table, i, past_len)
        hp, rows_p = _trunk_layer(hp, p_prompt[i], pos_p, i, attend_p, *lw)
        hs, rows_s = _trunk_layer(hs, p_sample[i], pos_s, i, attend_s, *lw)
        new_p.append(rows_p)
        new_s.append(rows_s)
    stack = lambda rows, j: jnp.stack([r[j] for r in rows])
    y_prompt = _rms_norm(hp, final_norm)
    y_sample = _rms_norm(hs, final_norm)
    return (y_prompt, y_sample,
            stack(new_p, 0), stack(new_p, 1), stack(new_p, 2), stack(new_p, 3),
            stack(new_s, 0), stack(new_s, 1), stack(new_s, 2), stack(new_s, 3))
```

```python
import functools
import math

import numpy as np
import jax
import jax.numpy as jnp
from jax import lax
from jax.experimental import pallas as pl
from jax.experimental.pallas import tpu as pltpu

F32 = jnp.float32
BF16 = jnp.bfloat16

D_MODEL = 1024
HEAD_DIM = 64
DIFF_HEADS = 4
DIFF_WIDTH = DIFF_HEADS * 2 * HEAD_DIM
MLA_HEADS = 8
MLA_NOPE = 64
MLA_ROPE = 32
MLA_V = 64
Q_LORA = 256
KV_LORA = 128
MLA_WIDTH = MLA_HEADS * MLA_V
D_FF = 2816
PLE_DIM = 256
NUM_BUCKETS = 32
MAX_DISTANCE = 128
ROPE_THETA = 10000.0
EPS = 1e-6
NEG = -1e30
LOG2E = 1.4426950408889634
DIFF_SCALE = HEAD_DIM ** -0.5
MLA_SCALE = (MLA_NOPE + MLA_ROPE) ** -0.5
LAMBDA_INIT = 0.8 - 0.6 * math.exp(-0.3 * 0)
MAIN_WIDTH = 3 * DIFF_WIDTH + Q_LORA + KV_LORA

LANE = 128
VMEM_LIMIT = 56 * 1024 * 1024

PROJ_ROWS = 512
POST_ROWS = 256
ATT_T = 256
MLA_GROUP = 4
DEC_PAGES = 8
DEC_ROWS = 2 * DIFF_HEADS


def _rms(x, g):
    return x * lax.rsqrt(jnp.mean(x * x, axis=-1, keepdims=True) + EPS) * g


def _bucket_np(rel):
    n = np.maximum(rel, 0)
    max_exact = NUM_BUCKETS // 2
    nf = np.maximum(n, 1).astype(np.float32)
    large = max_exact + (np.log(nf / np.float32(max_exact)) / np.float32(math.log(MAX_DISTANCE / max_exact))
                         * np.float32(NUM_BUCKETS - max_exact)).astype(np.int32)
    large = np.minimum(large, NUM_BUCKETS - 1)
    return np.where(n < max_exact, n, large).astype(np.int32)


def _const_spec(shape):
    nd = len(shape)
    return pl.BlockSpec(shape, lambda *_: (0,) * nd)


def _proj_kernel(x_ref, cos_ref, sin_ref, g_ref, win_ref, wkr_ref, qn_ref, kvn_ref, wqn_ref, wqr_ref, wuk_ref,
                 qd_ref, kd_ref, kdb_ref, vd_ref, vdb_ref, ckv_ref, kr_ref, kc_ref, qcat_ref):
    hn = _rms(x_ref[...], g_ref[...]).astype(BF16)
    z = jnp.dot(hn, win_ref[...], preferred_element_type=F32)
    qd = z[:, 0:DIFF_WIDTH]
    kd = z[:, DIFF_WIDTH:2 * DIFF_WIDTH]
    vd = z[:, 2 * DIFF_WIDTH:3 * DIFF_WIDTH]
    cq = z[:, 3 * DIFF_WIDTH:3 * DIFF_WIDTH + Q_LORA]
    ckv = z[:, 3 * DIFF_WIDTH + Q_LORA:MAIN_WIDTH]
    qd_ref[...] = (qd * (DIFF_SCALE * LOG2E)).astype(BF16)
    kd_ref[...] = kd
    kdb_ref[...] = kd.astype(BF16)
    vd_ref[...] = vd
    vdb_ref[...] = vd.astype(BF16)

    cosf = cos_ref[...]
    sinf = sin_ref[...]
    zkr = jnp.dot(hn, wkr_ref[...], preferred_element_type=F32)
    kr = zkr[:, :LANE] * cosf + zkr[:, LANE:] * sinf
    kr_ref[...] = kr[:, :MLA_ROPE]
    c = _rms(ckv, kvn_ref[...])
    ckv_ref[...] = c
    kc_ref[:, :LANE] = c.astype(BF16)
    kc_ref[:, LANE:] = kr.astype(BF16)

    cqn = _rms(cq, qn_ref[...]).astype(BF16)
    qnope = jnp.dot(cqn, wqn_ref[...], preferred_element_type=F32)
    qlat = jnp.dot(qnope.astype(BF16), wuk_ref[...], preferred_element_type=F32)
    zr = jnp.dot(cqn, wqr_ref[...], preferred_element_type=F32)
    half = MLA_HEADS * LANE
    for h in range(MLA_HEADS):
        sl = slice(h * LANE, (h + 1) * LANE)
        qcat_ref[h, :, :LANE] = (qlat[:, sl] * (MLA_SCALE * LOG2E)).astype(BF16)
        qr = zr[:, sl] * cosf + zr[:, half + h * LANE:half + (h + 1) * LANE] * sinf
        qcat_ref[h, :, LANE:] = (qr * (MLA_SCALE * LOG2E)).astype(BF16)


def _project(x, cosf, sinf, w, rows):
    n = x.shape[0]
    row = lambda width: pl.BlockSpec((rows, width), lambda i: (i, 0))
    out_shape = (
        jax.ShapeDtypeStruct((n, DIFF_WIDTH), BF16),
        jax.ShapeDtypeStruct((n, DIFF_WIDTH), F32),
        jax.ShapeDtypeStruct((n, DIFF_WIDTH), BF16),
        jax.ShapeDtypeStruct((n, DIFF_WIDTH), F32),
        jax.ShapeDtypeStruct((n, DIFF_WIDTH), BF16),
        jax.ShapeDtypeStruct((n, KV_LORA), F32),
        jax.ShapeDtypeStruct((n, MLA_ROPE), F32),
        jax.ShapeDtypeStruct((n, 2 * LANE), BF16),
        jax.ShapeDtypeStruct((MLA_HEADS, n, 2 * LANE), BF16),
    )
    out_specs = (row(DIFF_WIDTH), row(DIFF_WIDTH), row(DIFF_WIDTH), row(DIFF_WIDTH), row(DIFF_WIDTH),
                 row(KV_LORA), row(MLA_ROPE), row(2 * LANE),
                 pl.BlockSpec((MLA_HEADS, rows, 2 * LANE), lambda i: (0, i, 0)))
    weights = (w["norm_attn"], w["w_in_main"], w["w_kr"], w["q_norm"], w["kv_norm"], w["wq_nope"], w["wq_rope"],
               w["wuk_bd"])
    in_specs = [row(D_MODEL), row(LANE), row(LANE)] + [_const_spec(a.shape) for a in weights]
    return pl.pallas_call(
        _proj_kernel,
        grid=(n // rows,),
        in_specs=in_specs,
        out_specs=out_specs,
        out_shape=out_shape,
        compiler_params=pltpu.CompilerParams(dimension_semantics=("arbitrary",), vmem_limit_bytes=VMEM_LIMIT),
        name="proj",
    )(x, cosf, sinf, *weights)


def _flash_step(qs, k, v, bias, m_sc, l_sc, acc_sc):
    s = lax.dot_general(qs, k, (((1,), (1,)), ((), ())), preferred_element_type=F32)
    if bias is not None:
        s = s + bias
    m_prev = m_sc[...]
    m_new = jnp.maximum(m_prev, jnp.max(s, axis=-1, keepdims=True))
    alpha = jnp.exp2(m_prev - m_new)
    p = jnp.exp2(s - m_new)
    l_sc[...] = alpha * l_sc[...] + jnp.sum(p, axis=-1, keepdims=True)
    acc_sc[...] = alpha * acc_sc[...] + jnp.dot(p.astype(BF16), v, preferred_element_type=F32)
    m_sc[...] = m_new


def _diff_lambda(lam_ref):
    lp = lam_ref[...]
    a = jnp.sum(lp[0:1] * lp[1:2], axis=-1, keepdims=True)
    b = jnp.sum(lp[2:3] * lp[3:4], axis=-1, keepdims=True)
    return jnp.exp(a) - jnp.exp(b) + LAMBDA_INIT


def _diff_attn_kernel(tab_ref, q_ref, k_ref, v_ref, idx_ref, lam_ref, subln_ref, o_ref,
                      bias_sc, m_sc, l_sc, acc_sc):
    t = ATT_T
    h = pl.program_id(0)
    i = pl.program_id(1)

    @pl.when(i == 0)
    def _():
        idx = idx_ref[...]
        val = jnp.zeros(idx.shape, F32)
        for bkt in range(NUM_BUCKETS):
            val = jnp.where(idx == bkt, tab_ref[bkt, h], val)
        val = (val - tab_ref[NUM_BUCKETS - 1, h]) * LOG2E
        val = jnp.where(idx < 0, NEG, val)
        for w in range(2):
            tile = val[:, w * t:(w + 1) * t]
            bias_sc[w, :t, :] = tile
            bias_sc[w, t:, :] = tile

    q = q_ref[...]
    lane = lax.broadcasted_iota(jnp.int32, q.shape, 1)
    zero = jnp.zeros_like(q)
    qs = jnp.concatenate([jnp.where(lane < HEAD_DIM, q, zero), jnp.where(lane >= HEAD_DIM, q, zero)], axis=0)

    m_sc[...] = jnp.full(m_sc.shape, NEG, F32)
    l_sc[...] = jnp.zeros(l_sc.shape, F32)
    acc_sc[...] = jnp.zeros(acc_sc.shape, F32)

    def tile(j):
        rows = pl.ds(pl.multiple_of(j * t, t), t)
        return k_ref[rows, :], v_ref[rows, :]

    def far(j, carry):
        k, v = tile(j)
        _flash_step(qs, k, v, None, m_sc, l_sc, acc_sc)
        return carry

    lax.fori_loop(0, jnp.maximum(i - 1, 0), far, 0)

    @pl.when(i >= 1)
    def _():
        k, v = tile(i - 1)
        _flash_step(qs, k, v, bias_sc[0], m_sc, l_sc, acc_sc)

    k, v = tile(i)
    _flash_step(qs, k, v, bias_sc[1], m_sc, l_sc, acc_sc)

    o = acc_sc[...] / l_sc[...]
    od = o[:t] - _diff_lambda(lam_ref) * o[t:]
    od = _rms(od, subln_ref[...]) * (1.0 - LAMBDA_INIT)
    o_ref[...] = od.astype(o_ref.dtype)


def _prompt_diff_attention(qd_b, kd_b, vd_b, near_idx, rel_bias, diff_lambda, diff_subln):
    s = qd_b.shape[0]
    t = ATT_T
    return pl.pallas_call(
        _diff_attn_kernel,
        grid=(DIFF_HEADS, s // t),
        in_specs=[
            pl.BlockSpec(memory_space=pltpu.SMEM),
            pl.BlockSpec((t, LANE), lambda h, i: (i, h)),
            pl.BlockSpec((s, LANE), lambda h, i: (0, h)),
            pl.BlockSpec((s, LANE), lambda h, i: (0, h)),
            _const_spec(near_idx.shape),
            _const_spec(diff_lambda.shape),
            _const_spec(diff_subln.shape),
        ],
        out_specs=pl.BlockSpec((t, LANE), lambda h, i: (i, h)),
        out_shape=jax.ShapeDtypeStruct((s, DIFF_WIDTH), BF16),
        scratch_shapes=[
            pltpu.VMEM((2, 2 * t, t), F32),
            pltpu.VMEM((2 * t, 1), F32),
            pltpu.VMEM((2 * t, 1), F32),
            pltpu.VMEM((2 * t, LANE), F32),
        ],
        compiler_params=pltpu.CompilerParams(dimension_semantics=("arbitrary", "arbitrary"),
                                             vmem_limit_bytes=VMEM_LIMIT),
        name="prompt_diff_attn",
    )(rel_bias, qd_b, kd_b, vd_b, near_idx, diff_lambda, diff_subln)


def _mla_attn_kernel(q_ref, kc_ref, o_ref, m_sc, l_sc, acc_sc):
    t = ATT_T
    g = MLA_GROUP
    i = pl.program_id(1)
    qs = q_ref[...].reshape(g * t, 2 * LANE)

    m_sc[...] = jnp.full(m_sc.shape, NEG, F32)
    l_sc[...] = jnp.zeros(l_sc.shape, F32)
    acc_sc[...] = jnp.zeros(acc_sc.shape, F32)

    def tile(j):
        kc = kc_ref[pl.ds(pl.multiple_of(j * t, t), t), :]
        return kc, kc[:, :KV_LORA]

    def far(j, carry):
        k, v = tile(j)
        _flash_step(qs, k, v, None, m_sc, l_sc, acc_sc)
        return carry

    lax.fori_loop(0, i, far, 0)

    row = lax.broadcasted_iota(jnp.int32, (g * t, t), 0) & (t - 1)
    col = lax.broadcasted_iota(jnp.int32, (g * t, t), 1)
    mask = jnp.where(col > row, NEG, 0.0).astype(F32)
    k, v = tile(i)
    _flash_step(qs, k, v, mask, m_sc, l_sc, acc_sc)

    o = acc_sc[...] / l_sc[...]
    for hh in range(g):
        o_ref[:, hh * KV_LORA:(hh + 1) * KV_LORA] = o[hh * t:(hh + 1) * t].astype(o_ref.dtype)


def _prompt_mla_attention(qcat_b, kc_b):
    s = kc_b.shape[0]
    t = ATT_T
    g = MLA_GROUP
    return pl.pallas_call(
        _mla_attn_kernel,
        grid=(MLA_HEADS // g, s // t),
        in_specs=[
            pl.BlockSpec((g, t, 2 * LANE), lambda gi, i: (gi, i, 0)),
            pl.BlockSpec((s, 2 * LANE), lambda gi, i: (0, 0)),
        ],
        out_specs=pl.BlockSpec((t, g * KV_LORA), lambda gi, i: (i, gi)),
        out_shape=jax.ShapeDtypeStruct((s, MLA_HEADS * KV_LORA), BF16),
        scratch_shapes=[
            pltpu.VMEM((g * t, 1), F32),
            pltpu.VMEM((g * t, 1), F32),
            pltpu.VMEM((g * t, KV_LORA), F32),
        ],
        compiler_params=pltpu.CompilerParams(dimension_semantics=("arbitrary", "arbitrary"),
                                             vmem_limit_bytes=VMEM_LIMIT),
        name="prompt_mla_attn",
    )(qcat_b, kc_b)


def _online(m, l, acc, s, pv):
    m_new = jnp.maximum(m, jnp.max(s, axis=-1, keepdims=True))
    alpha = jnp.exp2(m - m_new)
    p = jnp.exp2(s - m_new)
    return m_new, alpha * l + jnp.sum(p, axis=-1, keepdims=True), alpha * acc + pv(p)


def _dec_attn_kernel(pt_ref, tab_ref, qd_ref, qc_ref, ks_ref, vs_ref, cs_ref, rs_ref, lastidx_ref, lam_ref,
                     subln_ref, ck_hbm, cv_hbm, cc_hbm, cr_hbm, od_ref, ol_ref,
                     kbuf, vbuf, cbuf, rbuf, sems, bias_sc, mask_sc, *, n_pages):
    pg = DEC_PAGES
    n_chunks = n_pages // pg
    b = pl.program_id(0)
    nb = pl.num_programs(0)
    nt = (((1,), (1,)), ((), ()))
    width = ck_hbm.shape[1]

    def page_copies(seq, chunk, slot):
        cps = []
        for p in range(pg):
            page = pt_ref[seq * n_pages + chunk * pg + p]
            cps.append(pltpu.make_async_copy(ck_hbm.at[page], kbuf.at[slot, p], sems.at[slot, 0]))
            cps.append(pltpu.make_async_copy(cv_hbm.at[page], vbuf.at[slot, p], sems.at[slot, 1]))
            cps.append(pltpu.make_async_copy(cc_hbm.at[page], cbuf.at[slot, p], sems.at[slot, 2]))
            cps.append(pltpu.make_async_copy(cr_hbm.at[page], rbuf.at[slot, p], sems.at[slot, 3]))
        return cps

    def start(seq, chunk, slot):
        for cp in page_copies(seq, chunk, slot):
            cp.start()

    def wait(seq, chunk, slot):
        for cp in page_copies(seq, chunk, slot):
            cp.wait()

    row = lax.broadcasted_iota(jnp.int32, (DEC_ROWS, width), 0)
    col = lax.broadcasted_iota(jnp.int32, (DEC_ROWS, width), 1)

    @pl.when(b == 0)
    def _():
        start(0, 0, 0)
        own = (col & (DIFF_HEADS - 1)) == (row >> 1)
        mask_sc[...] = jnp.where(own, 0.0, NEG)
        idx = lastidx_ref[...]
        bias = jnp.zeros((DEC_ROWS, width), F32)
        for h in range(DIFF_HEADS):
            val = jnp.zeros(idx.shape, F32)
            for bkt in range(NUM_BUCKETS):
                val = jnp.where(idx == bkt, tab_ref[bkt, h], val)
            val = (val - tab_ref[NUM_BUCKETS - 1, h]) * LOG2E
            bias = jnp.where((row >> 1) == h, val, bias)
        bias_sc[...] = jnp.where(own, bias, NEG)

    q8 = qd_ref[0]
    qc = qc_ref[0]
    ql = qc[:, :KV_LORA]
    qr = qc[:, KV_LORA:KV_LORA + MLA_ROPE]

    def compute(chunk, slot, carry):
        m_d, l_d, a_d, m_m, l_m, a_m = carry
        is_last = chunk == n_chunks - 1
        mask = mask_sc[...]
        s_d = [lax.dot_general(q8, kbuf[slot, p], nt, preferred_element_type=F32) + mask for p in range(pg - 1)]
        s_d.append(lax.dot_general(q8, kbuf[slot, pg - 1], nt, preferred_element_type=F32)
                   + jnp.where(is_last, bias_sc[...], mask))
        s_d = jnp.concatenate(s_d, axis=1)
        s_m = [lax.dot_general(ql, cbuf[slot, p], nt, preferred_element_type=F32)
               + jnp.dot(qr, rbuf[slot, p], preferred_element_type=F32) for p in range(pg)]
        s_m = jnp.concatenate(s_m, axis=1)

        def pv_d(p):
            out = jnp.dot(p[:, :width], vbuf[slot, 0], preferred_element_type=F32)
            for pp in range(1, pg):
                out = out + jnp.dot(p[:, pp * width:(pp + 1) * width], vbuf[slot, pp], preferred_element_type=F32)
            return out

        def pv_m(p):
            out = jnp.dot(p[:, :LANE], cbuf[slot, 0], preferred_element_type=F32)
            for pp in range(1, pg):
                out = out + jnp.dot(p[:, pp * LANE:(pp + 1) * LANE], cbuf[slot, pp], preferred_element_type=F32)
            return out

        m_d, l_d, a_d = _online(m_d, l_d, a_d, s_d, pv_d)
        m_m, l_m, a_m = _online(m_m, l_m, a_m, s_m, pv_m)
        return m_d, l_d, a_d, m_m, l_m, a_m

    def pair(cc, carry):
        c0 = 2 * cc
        start(b, c0 + 1, 1)
        wait(b, c0, 0)
        carry = compute(c0, 0, carry)
        c1 = c0 + 1

        @pl.when(c1 + 1 < n_chunks)
        def _():
            start(b, c1 + 1, 0)

        @pl.when(jnp.logical_and(c1 + 1 == n_chunks, b + 1 < nb))
        def _():
            start(b + 1, 0, 0)

        wait(b, c1, 1)
        return compute(c1, 1, carry)

    carry = (jnp.full((DEC_ROWS, 1), NEG, F32), jnp.zeros((DEC_ROWS, 1), F32), jnp.zeros((DEC_ROWS, LANE), F32),
             jnp.full((MLA_HEADS, 1), NEG, F32), jnp.zeros((MLA_HEADS, 1), F32),
             jnp.zeros((MLA_HEADS, KV_LORA), F32))
    m_d, l_d, a_d, m_m, l_m, a_m = lax.fori_loop(0, n_chunks // 2, pair, carry)

    self_bias = jnp.zeros((DEC_ROWS, 1), F32)
    for h in range(DIFF_HEADS):
        val = (tab_ref[0, h] - tab_ref[NUM_BUCKETS - 1, h]) * LOG2E
        self_bias = jnp.where((row[:, :1] >> 1) == h, val, self_bias)
    s_self = jnp.sum(q8 * ks_ref[0], axis=-1, keepdims=True) + self_bias
    v_self = vs_ref[0]
    m_d, l_d, a_d = _online(m_d, l_d, a_d, s_self, lambda p: p * v_self)
    kc_self = jnp.concatenate([cs_ref[0], rs_ref[0]], axis=-1)
    s_self = jnp.sum(qc * kc_self, axis=-1, keepdims=True)
    c_self = cs_ref[0]
    m_m, l_m, a_m = _online(m_m, l_m, a_m, s_self, lambda p: p * c_self)

    o = a_d / l_d
    od = o - _diff_lambda(lam_ref) * pltpu.roll(o, DEC_ROWS - 1, 0)
    od_ref[0] = _rms(od, subln_ref[...]) * (1.0 - LAMBDA_INIT)
    ol_ref[0] = a_m / l_m


def _sample_attention(page_table, rel_bias, qd_s, qc_s, k_s, v_s, c_s, r_s, last_idx, diff_lambda, diff_subln,
                      cache_k, cache_v, cache_c, cache_r):
    nb, n_pages = page_table.shape
    width = cache_k.shape[1]
    page = cache_c.shape[1]
    pg = DEC_PAGES
    seq3 = lambda a: pl.BlockSpec((1,) + a.shape[1:], lambda b, pt: (b, 0, 0))
    const = lambda a: pl.BlockSpec(a.shape, lambda b, pt: (0,) * a.ndim)
    hbm = pl.BlockSpec(memory_space=pl.ANY)
    grid_spec = pltpu.PrefetchScalarGridSpec(
        num_scalar_prefetch=1,
        grid=(nb,),
        in_specs=[
            pl.BlockSpec(memory_space=pltpu.SMEM),
            seq3(qd_s), seq3(qc_s), seq3(k_s), seq3(v_s), seq3(c_s), seq3(r_s),
            const(last_idx), const(diff_lambda), const(diff_subln),
            hbm, hbm, hbm, hbm,
        ],
        out_specs=(pl.BlockSpec((1, DEC_ROWS, LANE), lambda b, pt: (b, 0, 0)),
                   pl.BlockSpec((1, MLA_HEADS, KV_LORA), lambda b, pt: (b, 0, 0))),
        scratch_shapes=[
            pltpu.VMEM((2, pg, width, LANE), F32),
            pltpu.VMEM((2, pg, width, LANE), F32),
            pltpu.VMEM((2, pg, page, KV_LORA), F32),
            pltpu.VMEM((2, pg, MLA_ROPE, page), F32),
            pltpu.SemaphoreType.DMA((2, 4)),
            pltpu.VMEM((DEC_ROWS, width), F32),
            pltpu.VMEM((DEC_ROWS, width), F32),
        ],
    )
    return pl.pallas_call(
        functools.partial(_dec_attn_kernel, n_pages=n_pages),
        grid_spec=grid_spec,
        out_shape=(jax.ShapeDtypeStruct((nb, DEC_ROWS, LANE), F32),
                   jax.ShapeDtypeStruct((nb, MLA_HEADS, KV_LORA), F32)),
        compiler_params=pltpu.CompilerParams(dimension_semantics=("arbitrary",), vmem_limit_bytes=VMEM_LIMIT),
        name="sample_attn",
    )(page_table.reshape(-1), rel_bias, qd_s, qc_s, k_s, v_s, c_s, r_s, last_idx, diff_lambda, diff_subln,
      cache_k, cache_v, cache_c, cache_r)


def _post_kernel(x_ref, od_ref, ol_ref, p_ref, wuv_ref, wout_ref, nffn_ref, wg_ref, wu_ref, wd_ref,
                 nple_ref, wpg_ref, wpp_ref, fin_ref, y_ref):
    o_mla = jnp.dot(ol_ref[...].astype(BF16), wuv_ref[...], preferred_element_type=F32)
    h = (x_ref[...]
         + jnp.dot(od_ref[...].astype(BF16), wout_ref[:DIFF_WIDTH, :], preferred_element_type=F32)
         + jnp.dot(o_mla.astype(BF16), wout_ref[DIFF_WIDTH:, :], preferred_element_type=F32))
    hn = _rms(h, nffn_ref[...]).astype(BF16)
    gate = jnp.dot(hn, wg_ref[...], preferred_element_type=F32)
    up = jnp.dot(hn, wu_ref[...], preferred_element_type=F32)
    act = (gate / (1.0 + jnp.exp(-gate))) * up
    h = h + jnp.dot(act.astype(BF16), wd_ref[...], preferred_element_type=F32)
    hn = _rms(h, nple_ref[...]).astype(BF16)
    pg = jnp.dot(hn, wpg_ref[...], preferred_element_type=F32)
    pe = jnp.dot(p_ref[...].astype(BF16), wpp_ref[...], preferred_element_type=F32)
    h = h + pe / (1.0 + jnp.exp(-pg))
    y_ref[...] = _rms(h, fin_ref[...])


def _post(x, od, ol, p, w, rows):
    n = x.shape[0]
    row = lambda a: pl.BlockSpec((rows, a.shape[1]), lambda i: (i, 0))
    weights = (w["wuv_bd"], w["w_out"], w["norm_ffn"], w["w_gate"], w["w_up"], w["w_down"], w["norm_ple"],
               w["w_ple_gate"], w["w_ple_proj"], w["final_norm"])
    wspec = lambda a: pl.BlockSpec(a.shape, lambda i: (0,) * a.ndim, pipeline_mode=pl.Buffered(1))
    return pl.pallas_call(
        _post_kernel,
        grid=(n // rows,),
        in_specs=[row(x), row(od), row(ol), row(p)] + [wspec(a) for a in weights],
        out_specs=pl.BlockSpec((rows, D_MODEL), lambda i: (i, 0)),
        out_shape=jax.ShapeDtypeStruct((n, D_MODEL), F32),
        compiler_params=pltpu.CompilerParams(dimension_semantics=("arbitrary",), vmem_limit_bytes=VMEM_LIMIT),
        name="post",
    )(x, od, ol, p, *weights)


def _rot(wr):
    half = wr.shape[-1] // 2
    return jnp.concatenate([-wr[..., half:], wr[..., :half]], axis=-1)


def _pad_lanes(a):
    return jnp.pad(a, [(0, 0)] * (a.ndim - 1) + [(0, LANE - a.shape[-1])])


def _prepare_weights(norm_attn, w_in, mla_q_norm, w_q_up, mla_kv_norm, w_uk, w_uv, w_out, norm_ffn,
                     w_ffn_gate, w_ffn_up, w_ffn_down, norm_ple, w_ple_gate, w_ple_proj, final_norm):
    w_kr = w_in[:, MAIN_WIDTH:]
    w_kr = jnp.concatenate([_pad_lanes(w_kr), _pad_lanes(_rot(w_kr))], axis=-1)
    wq = w_q_up.reshape(Q_LORA, MLA_HEADS, MLA_NOPE + MLA_ROPE)
    wq_nope = wq[:, :, :MLA_NOPE].reshape(Q_LORA, MLA_HEADS * MLA_NOPE)
    wr = wq[:, :, MLA_NOPE:]
    wq_rope = jnp.concatenate([_pad_lanes(wr).reshape(Q_LORA, MLA_HEADS * LANE),
                               _pad_lanes(_rot(wr)).reshape(Q_LORA, MLA_HEADS * LANE)], axis=-1)
    eye = jnp.eye(MLA_HEADS, dtype=F32)
    wuk_t = jnp.transpose(w_uk, (1, 2, 0))
    wuk_bd = (eye[:, None, :, None] * wuk_t[:, :, None, :]).reshape(MLA_HEADS * MLA_NOPE, MLA_HEADS * KV_LORA)
    wuv_t = jnp.transpose(w_uv, (1, 0, 2))
    wuv_bd = (eye[:, None, :, None] * wuv_t[:, :, None, :]).reshape(MLA_HEADS * KV_LORA, MLA_WIDTH)
    vec = lambda a: a.reshape(1, -1)
    return {
        "norm_attn": vec(norm_attn), "w_in_main": w_in[:, :MAIN_WIDTH].astype(BF16), "w_kr": w_kr.astype(BF16),
        "q_norm": vec(mla_q_norm), "kv_norm": vec(mla_kv_norm),
        "wq_nope": wq_nope.astype(BF16), "wq_rope": wq_rope.astype(BF16), "wuk_bd": wuk_bd.astype(BF16),
        "wuv_bd": wuv_bd.astype(BF16), "w_out": w_out.astype(BF16), "norm_ffn": vec(norm_ffn),
        "w_gate": w_ffn_gate.astype(BF16), "w_up": w_ffn_up.astype(BF16), "w_down": w_ffn_down.astype(BF16),
        "norm_ple": vec(norm_ple), "w_ple_gate": w_ple_gate.astype(BF16), "w_ple_proj": w_ple_proj.astype(BF16),
        "final_norm": vec(final_norm),
    }


def _rope_tables(pos):
    half = MLA_ROPE // 2
    inv = ROPE_THETA ** (-jnp.arange(half, dtype=F32) * 2.0 / MLA_ROPE)
    ang = pos.astype(F32)[:, None] * inv[None, :]
    cos = jnp.cos(ang)
    sin = jnp.sin(ang)
    return (_pad_lanes(jnp.concatenate([cos, cos], axis=-1)), _pad_lanes(jnp.concatenate([sin, sin], axis=-1)))


def kernel(x_prompt, x_sample, p_prompt, p_sample, cache_diff_k, cache_diff_v, cache_mla_ckv, cache_mla_krope,
           page_table, rel_bias, norm_attn, w_in, diff_lambda, diff_subln, mla_q_norm, w_q_up, mla_kv_norm,
           w_uk, w_uv, w_out, norm_ffn, w_ffn_gate, w_ffn_up, w_ffn_down, norm_ple, w_ple_gate, w_ple_proj,
           final_norm):
    depth = norm_attn.shape[0]
    assert depth == 1 and x_prompt.shape[0] == 1 and x_sample.shape[1] == 1
    s = x_prompt.shape[1]
    nb = x_sample.shape[0]
    n_pool, page = cache_diff_k.shape[1], cache_diff_k.shape[2]
    n_pages = page_table.shape[1]
    past_len = n_pages * page
    t = ATT_T
    assert s % t == 0 and s % PROJ_ROWS == 0 and s % POST_ROWS == 0 and n_pages % (2 * DEC_PAGES) == 0

    w = _prepare_weights(norm_attn[0], w_in[0], mla_q_norm[0], w_q_up[0], mla_kv_norm[0], w_uk[0], w_uv[0],
                         w_out[0], norm_ffn[0], w_ffn_gate[0], w_ffn_up[0], w_ffn_down[0], norm_ple[0],
                         w_ple_gate[0], w_ple_proj[0], final_norm)
    lam = diff_lambda[0]
    subln = diff_subln[0].reshape(1, -1)

    rel = t + np.arange(t)[:, None] - np.arange(2 * t)[None, :]
    near_idx = np.where(rel >= 0, _bucket_np(rel), -1).astype(np.int32)
    far = NUM_BUCKETS - 1
    assert (_bucket_np(np.arange(t + 1, 4 * t)) == far).all()
    rel_s = past_len - np.arange(past_len)
    assert (_bucket_np(rel_s[:past_len - page]) == far).all()
    last_idx = np.repeat(_bucket_np(rel_s[past_len - page:]), DIFF_HEADS).reshape(1, page * DIFF_HEADS)

    xp = x_prompt[0]
    cos_p, sin_p = _rope_tables(jnp.arange(s))
    qd_b, kd, kd_b, vd, vd_b, ckv, kr, kc_b, qcat_b = _project(xp, cos_p, sin_p, w, PROJ_ROWS)
    od_p = _prompt_diff_attention(qd_b, kd_b, vd_b, jnp.asarray(near_idx), rel_bias, lam, subln)
    ol_p = _prompt_mla_attention(qcat_b, kc_b)
    y_p = _post(xp, od_p, ol_p, p_prompt[0, 0], w, POST_ROWS)

    xs = x_sample[:, 0]
    cos_s, sin_s = _rope_tables(jnp.full((nb,), past_len))
    qd_sb, kd_s, _, vd_s, _, ckv_s, kr_s, kc_sb, qcat_sb = _project(xs, cos_s, sin_s, w, nb)
    map_mask = (np.arange(LANE)[None, :] // HEAD_DIM == np.arange(2)[:, None]).astype(np.float32)
    q8 = (qd_sb.astype(F32).reshape(nb, DIFF_HEADS, 1, LANE) * map_mask[None, None]).reshape(nb, DEC_ROWS, LANE)
    per_row = lambda a: jnp.repeat(a.reshape(nb, DIFF_HEADS, LANE), 2, axis=1)
    od_s, ol_s = _sample_attention(
        page_table, rel_bias, q8,
        jnp.transpose(qcat_sb, (1, 0, 2)).astype(F32),
        per_row(kd_s), per_row(vd_s), ckv_s.reshape(nb, 1, KV_LORA), _pad_lanes(kr_s).reshape(nb, 1, LANE),
        jnp.asarray(last_idx), lam, subln,
        cache_diff_k[0].reshape(n_pool, page * DIFF_HEADS, LANE),
        cache_diff_v[0].reshape(n_pool, page * DIFF_HEADS, LANE),
        cache_mla_ckv[0], jnp.swapaxes(cache_mla_krope[0], 1, 2))
    od_s = od_s[:, 0::2, :].reshape(nb, DIFF_WIDTH)
    y_s = _post(xs, od_s, ol_s.reshape(nb, MLA_HEADS * KV_LORA), p_sample[0, :, 0], w, nb)

    hd = (DIFF_HEADS, 2 * HEAD_DIM)
    return (y_p[None], y_s[:, None],
            kd.reshape(1, 1, s, *hd), vd.reshape(1, 1, s, *hd), ckv.reshape(1, 1, s, KV_LORA),
            kr.reshape(1, 1, s, MLA_ROPE),
            kd_s.reshape(1, nb, 1, *hd), vd_s.reshape(1, nb, 1, *hd), ckv_s.reshape(1, nb, 1, KV_LORA),
            kr_s.reshape(1, nb, 1, MLA_ROPE))
```
